```python
import math
import jax
import jax.numpy as jnp
from jax import lax
import numpy as np

D_MODEL = 1024
BATCH = 16
SEQ = 2048
DEPTH = 4

N_MIXERS = 2
N_A = (DEPTH + 1) // 2
N_B = DEPTH // 2

GDN_DK = 128
GDN_DV = 128
GDN_HK = D_MODEL // 128
GDN_HV = 2 * GDN_HK
QK_DIM = GDN_HK * GDN_DK
V_DIM = GDN_HV * GDN_DV
CONV_W = 5
CONV_CH = 2 * QK_DIM + V_DIM
A_IN_DIM = CONV_CH + V_DIM + 4 * GDN_HV
CHUNK = 64

ATT_DH = 64
ATT_HQ = D_MODEL // ATT_DH
ATT_HKV = 4
ATT_G = ATT_HQ // ATT_HKV
Q_DIM = ATT_HQ * ATT_DH
KV_DIM = ATT_HKV * ATT_DH
B_IN_DIM = Q_DIM + 2 * KV_DIM
WINDOW = 128
WB = 128

N_EXPERTS = 32
TOP_K = 4
D_FF = D_MODEL
SWIGLU_LIMIT = 7.0
SWIGLU_ALPHA = 1.702
MOE_BLOCK = 256

ALPHA_DN = (2 * DEPTH) ** 0.25
BETA_DN = (8 * DEPTH) ** -0.25
LN_EPS = 1e-5
RMS_EPS = 1e-6

kernel_name = 'hybrid_gdn_swa_moe_deepnorm'


def layer_norm(x, g, b):
    xf = x.astype(jnp.float32)
    mu = jnp.mean(xf, axis=-1, keepdims=True)
    xc = xf - mu
    var = jnp.mean(xc * xc, axis=-1, keepdims=True)
    y = xc * lax.rsqrt(var + LN_EPS) * g.astype(jnp.float32) + b.astype(jnp.float32)
    return y.astype(x.dtype)


def l2_normalize(x):
    return x * lax.rsqrt(jnp.sum(x * x, axis=-1, keepdims=True) + RMS_EPS)


def short_conv(u, w):
    c = u.shape[-1]
    return lax.conv_general_dilated(
        u, w.astype(u.dtype)[:, None, :], window_strides=(1,),
        padding=[(CONV_W // 2, CONV_W // 2)],
        dimension_numbers=('NWC', 'WIO', 'NWC'), feature_group_count=c)


def gated_delta_rule(q, k, v, g, beta):
    bsz, h, t, _ = k.shape
    n = t // CHUNK
    q = q.reshape(bsz, h, n, CHUNK, GDN_DK)
    k = k.reshape(bsz, h, n, CHUNK, GDN_DK)
    v = v.reshape(bsz, h, n, CHUNK, GDN_DV)
    g = g.reshape(bsz, h, n, CHUNK)
    beta = beta.reshape(bsz, h, n, CHUNK)
    gc = jnp.cumsum(g, axis=-1)
    lower = jnp.tril(jnp.ones((CHUNK, CHUNK), dtype=bool))
    decay = jnp.exp(jnp.where(lower, gc[..., :, None] - gc[..., None, :], -jnp.inf))
    eye = jnp.eye(CHUNK, dtype=k.dtype)
    kb = k * beta[..., None]
    strict = jnp.einsum('bhnid,bhnjd->bhnij', kb, k) * decay * (1.0 - eye)
    lhs = eye + strict
    rhs = jnp.concatenate([v * beta[..., None], kb * jnp.exp(gc)[..., None]], axis=-1)
    sol = lax.linalg.triangular_solve(lhs, rhs, left_side=True, lower=True, unit_diagonal=True)
    u, w = sol[..., :GDN_DV], sol[..., GDN_DV:]
    qk = jnp.einsum('bhnid,bhnjd->bhnij', q, k) * decay
    qg = q * jnp.exp(gc)[..., None]
    kd = k * jnp.exp(gc[..., -1:] - gc)[..., None]
    gl = jnp.exp(gc[..., -1])
    xs = (jnp.moveaxis(u, 2, 0), jnp.moveaxis(w, 2, 0), jnp.moveaxis(qk, 2, 0),
          jnp.moveaxis(qg, 2, 0), jnp.moveaxis(kd, 2, 0), jnp.moveaxis(gl, 2, 0))

    def step(state, inp):
        u_c, w_c, qk_c, qg_c, kd_c, gl_c = inp
        v_new = u_c - jnp.einsum('bhcd,bhde->bhce', w_c, state)
        o_c = jnp.einsum('bhcd,bhde->bhce', qg_c, state) + jnp.einsum('bhcs,bhse->bhce', qk_c, v_new)
        state = state * gl_c[..., None, None] + jnp.einsum('bhcd,bhce->bhde', kd_c, v_new)
        return state, o_c

    s0 = jnp.zeros((bsz, h, GDN_DK, GDN_DV), k.dtype)
    _, o = lax.scan(step, s0, xs)
    return jnp.moveaxis(o, 0, 2).reshape(bsz, h, t, GDN_DV)


def deltanet_mixer(x, w_in, conv_w, a_log, dt_bias, norm_w, w_out):
    f32 = jnp.float32
    bsz, s, _ = x.shape
    hcat = x @ w_in
    qkv = jax.nn.silu(short_conv(hcat[..., :CONV_CH], conv_w))
    z = hcat[..., CONV_CH:CONV_CH + V_DIM].reshape(bsz, s, GDN_HV, GDN_DV)
    gates = hcat[..., CONV_CH + V_DIM:].astype(f32).reshape(bsz, s, 2, 2, GDN_HV)
    q = qkv[..., :QK_DIM].astype(f32).reshape(bsz, s, GDN_HK, GDN_DK)
    k = qkv[..., QK_DIM:2 * QK_DIM].astype(f32).reshape(bsz, s, GDN_HK, GDN_DK)
    v = qkv[..., 2 * QK_DIM:].astype(f32).reshape(bsz, s, GDN_HV, GDN_DV)
    q = l2_normalize(q) * (GDN_DK ** -0.5)
    k = l2_normalize(k)
    rep = GDN_HV // GDN_HK
    q = jnp.transpose(jnp.repeat(q, rep, axis=2), (0, 2, 1, 3))
    k = jnp.transpose(jnp.repeat(k, rep, axis=2), (0, 2, 1, 3))
    v = jnp.transpose(v, (0, 2, 1, 3))
    beta = jax.nn.sigmoid(gates[:, :, :, 0])
    g = -jnp.exp(a_log.astype(f32)) * jax.nn.softplus(gates[:, :, :, 1] + dt_bias.astype(f32))
    beta = jnp.transpose(beta, (2, 0, 3, 1))
    g = jnp.transpose(g, (2, 0, 3, 1))
    o_fwd = gated_delta_rule(q, k, v, g[0], beta[0])
    flip = lambda a: jnp.flip(a, axis=2)
    o_bwd = flip(gated_delta_rule(flip(q), flip(k), flip(v), flip(g[1]), flip(beta[1])))
    o = jnp.transpose(o_fwd + o_bwd, (0, 2, 1, 3))
    o = o * lax.rsqrt(jnp.mean(o * o, axis=-1, keepdims=True) + RMS_EPS) * norm_w.astype(f32)
    o = o * jax.nn.silu(z.astype(f32))
    return o.astype(x.dtype).reshape(bsz, s, V_DIM) @ w_out


def alibi_slopes():
    return 2.0 ** (-8.0 * jnp.arange(1, ATT_HQ + 1, dtype=jnp.float32) / ATT_HQ)


def window_attention_mixer(x, w_in, b_in, sinks, w_out, b_out):
    f32 = jnp.float32
    bsz, s, _ = x.shape
    nb = s // WB
    hcat = x @ w_in + b_in
    q = hcat[..., :Q_DIM].reshape(bsz, nb, WB, ATT_HKV, ATT_G, ATT_DH)
    k = hcat[..., Q_DIM:Q_DIM + KV_DIM].reshape(bsz, s, ATT_HKV, ATT_DH)
    v = hcat[..., Q_DIM + KV_DIM:].reshape(bsz, s, ATT_HKV, ATT_DH)
    pad = ((0, 0), (WB, WB), (0, 0), (0, 0))
    kp, vp = jnp.pad(k, pad), jnp.pad(v, pad)
    idx = jnp.arange(nb)[:, None] * WB + jnp.arange(3 * WB)[None, :]
    kw = jnp.moveaxis(kp[:, idx], 1, 0)
    vw = jnp.moveaxis(vp[:, idx], 1, 0)
    qb = jnp.moveaxis(q, 1, 0)
    slopes = alibi_slopes().reshape(ATT_HKV, ATT_G)
    sink = sinks.astype(f32).reshape(ATT_HKV, ATT_G)[:, :, None]
    qi = jnp.arange(WB)
    sj = jnp.arange(3 * WB)
    dist = jnp.abs(qi[:, None] + WB - sj[None, :])
    bias = -slopes[:, :, None, None] * dist.astype(f32)
    scale = ATT_DH ** -0.5

    def block(args):
        qj, kj, vj, j = args
        key_pos = j * WB - WB + sj
        valid = (dist <= WINDOW) & ((key_pos >= 0) & (key_pos < s))[None, :]
        sc = jnp.einsum('bqhgd,bshd->bhgqs', qj, kj).astype(f32) * scale + bias
        sc = jnp.where(valid, sc, -jnp.inf)
        m = jnp.maximum(jnp.max(sc, axis=-1), sink)
        p = jnp.exp(sc - m[..., None])
        denom = jnp.sum(p, axis=-1) + jnp.exp(sink - m)
        o = jnp.einsum('bhgqs,bshd->bqhgd', p, vj.astype(f32))
        o = o / jnp.transpose(denom, (0, 3, 1, 2))[..., None]
        return o.astype(x.dtype)

    o = lax.map(block, (qb, kw, vw, jnp.arange(nb)))
    o = jnp.moveaxis(o, 0, 1).reshape(bsz, s, Q_DIM)
    return o @ w_out + b_out


def moe_ffn(x, router_w, router_b, w_up, b_up, w_down, b_down):
    bsz, s, d = x.shape
    n_tok = bsz * s
    n_rows = n_tok * TOP_K
    xf = x.reshape(n_tok, d)
    logits = (xf @ router_w + router_b).astype(jnp.float32)
    top_v, top_e = lax.top_k(logits, TOP_K)
    gates = jax.nn.softmax(top_v, axis=-1)
    e_flat = top_e.reshape(-1)
    g_flat = gates.reshape(-1)
    t_flat = jnp.arange(n_rows, dtype=jnp.int32) // TOP_K
    order = jnp.argsort(e_flat)
    e_sorted = e_flat[order]
    counts = jnp.bincount(e_flat, length=N_EXPERTS)
    padded = (counts + MOE_BLOCK - 1) // MOE_BLOCK * MOE_BLOCK
    starts = jnp.cumsum(counts) - counts
    pends = jnp.cumsum(padded)
    pstarts = pends - padded
    dest = pstarts[e_sorted] + (jnp.arange(n_rows) - starts[e_sorted])
    n_blocks = (n_rows + MOE_BLOCK - 1) // MOE_BLOCK + N_EXPERTS
    n_pad = n_blocks * MOE_BLOCK
    row_tok = jnp.full((n_pad,), n_tok, dtype=jnp.int32).at[dest].set(t_flat[order])
    row_gate = jnp.zeros((n_pad,), jnp.float32).at[dest].set(g_flat[order])
    blk_e = jnp.minimum(jnp.searchsorted(pends, jnp.arange(n_blocks) * MOE_BLOCK, side='right'),
                        N_EXPERTS - 1)
    x_pad = jnp.concatenate([xf, jnp.zeros((1, d), xf.dtype)], axis=0)

    def expert_block(args):
        tok, e = args
        hb = x_pad[tok] @ w_up[e] + b_up[e]
        gate = jnp.minimum(hb[:, :D_FF], SWIGLU_LIMIT)
        up = jnp.clip(hb[:, D_FF:], -SWIGLU_LIMIT, SWIGLU_LIMIT)
        act = gate * jax.nn.sigmoid(SWIGLU_ALPHA * gate) * (up + 1.0)
        return act @ w_down[e] + b_down[e]

    y = lax.map(expert_block, (row_tok.reshape(n_blocks, MOE_BLOCK), blk_e))
    y = y.reshape(n_pad, d).astype(jnp.float32) * row_gate[:, None]
    out = jax.ops.segment_sum(y, row_tok, num_segments=n_tok + 1)[:n_tok]
    return out.astype(x.dtype).reshape(bsz, s, d)


def setup_inputs(seed: int = 0) -> dict:
    key = jax.random.key(seed)
    ks = jax.random.split(key, 20)
    f32 = jnp.float32

    def nrm(k, shape, scale):
        return jax.random.normal(k, shape, f32) * scale

    x = nrm(ks[0], (BATCH, SEQ, D_MODEL), 1.0)
    a_w_in = nrm(ks[1], (N_A, D_MODEL, A_IN_DIM), D_MODEL ** -0.5)
    a_conv_w = nrm(ks[2], (N_A, CONV_W, CONV_CH), CONV_W ** -0.5)
    a_A_log = jnp.log(jax.random.uniform(ks[3], (N_A, 2, GDN_HV), f32, 1.0, 16.0))
    dt = jnp.exp(jax.random.uniform(ks[4], (N_A, 2, GDN_HV), f32, math.log(1e-3), math.log(1e-1)))
    a_dt_bias = dt + jnp.log(-jnp.expm1(-dt))
    a_norm_w = 1.0 + nrm(ks[5], (N_A, GDN_DV), 0.02)
    a_w_out = nrm(ks[6], (N_A, V_DIM, D_MODEL), (V_DIM ** -0.5) * BETA_DN)
    b_w_in = nrm(ks[7], (N_B, D_MODEL, B_IN_DIM), D_MODEL ** -0.5)
    b_b_in = nrm(ks[8], (N_B, B_IN_DIM), 0.02)
    b_sinks = nrm(ks[9], (N_B, ATT_HQ), 1.0)
    b_w_out = nrm(ks[10], (N_B, Q_DIM, D_MODEL), (Q_DIM ** -0.5) * BETA_DN)
    b_b_out = nrm(ks[11], (N_B, D_MODEL), 0.02)
    router_w = nrm(ks[12], (DEPTH, D_MODEL, N_EXPERTS), D_MODEL ** -0.5)
    router_b = nrm(ks[13], (DEPTH, N_EXPERTS), 0.01)
    exp_w_up = nrm(ks[14], (DEPTH, N_EXPERTS, D_MODEL, 2 * D_FF), D_MODEL ** -0.5)
    exp_b_up = nrm(ks[15], (DEPTH, N_EXPERTS, 2 * D_FF), 0.02)
    exp_w_down = nrm(ks[16], (DEPTH, N_EXPERTS, D_FF, D_MODEL), (D_FF ** -0.5) * BETA_DN)
    exp_b_down = nrm(ks[17], (DEPTH, N_EXPERTS, D_MODEL), 0.02)
    ln_g = 1.0 + nrm(ks[18], (DEPTH, 2, D_MODEL), 0.02)
    ln_b = nrm(ks[19], (DEPTH, 2, D_MODEL), 0.02)
    return {'x': x, 'a_w_in': a_w_in, 'a_conv_w': a_conv_w, 'a_A_log': a_A_log,
            'a_dt_bias': a_dt_bias, 'a_norm_w': a_norm_w, 'a_w_out': a_w_out,
            'b_w_in': b_w_in, 'b_b_in': b_b_in, 'b_sinks': b_sinks, 'b_w_out': b_w_out,
            'b_b_out': b_b_out, 'router_w': router_w, 'router_b': router_b,
            'exp_w_up': exp_w_up, 'exp_b_up': exp_b_up, 'exp_w_down': exp_w_down,
            'exp_b_down': exp_b_down, 'ln_g': ln_g, 'ln_b': ln_b}


def reference(x, a_w_in, a_conv_w, a_A_log, a_dt_bias, a_norm_w, a_w_out,
              b_w_in, b_b_in, b_sinks, b_w_out, b_b_out,
              router_w, router_b, exp_w_up, exp_b_up, exp_w_down, exp_b_down,
              ln_g, ln_b):
    for i in range(DEPTH):
        j = i // N_MIXERS
        if i % N_MIXERS == 0:
            h = deltanet_mixer(x, a_w_in[j], a_conv_w[j], a_A_log[j], a_dt_bias[j],
                               a_norm_w[j], a_w_out[j])
        else:
            h = window_attention_mixer(x, b_w_in[j], b_b_in[j], b_sinks[j], b_w_out[j], b_b_out[j])
        x = layer_norm(ALPHA_DN * x + h, ln_g[i, 0], ln_b[i, 0])
        h = moe_ffn(x, router_w[i], router_b[i], exp_w_up[i], exp_b_up[i],
                    exp_w_down[i], exp_b_down[i])
        x = layer_norm(ALPHA_DN * x + h, ln_g[i, 1], ln_b[i, 1])
    return x
```

```python
import functools

import jax
import jax.numpy as jnp
from jax import lax
from jax.experimental import pallas as pl
from jax.experimental.pallas import tpu as pltpu

F32 = jnp.float32
BF16 = jnp.bfloat16
I32 = jnp.int32

D_MODEL = 1024
DEPTH = 4
GDN_DK = 128
GDN_DV = 128
GDN_HK = 8
GDN_HV = 16
QK_DIM = GDN_HK * GDN_DK
V_DIM = GDN_HV * GDN_DV
CONV_W = 5
CONV_CH = 2 * QK_DIM + V_DIM
CHUNK = 64
ATT_DH = 64
ATT_HQ = 16
ATT_HKV = 4
ATT_G = ATT_HQ // ATT_HKV
Q_DIM = ATT_HQ * ATT_DH
KV_DIM = ATT_HKV * ATT_DH
WB = 128
WINDOW = 128
N_EXPERTS = 32
TOP_K = 4
D_FF = D_MODEL
SWIGLU_LIMIT = 7.0
SWIGLU_ALPHA = 1.702
MOE_BLOCK = 256
ALPHA_DN = (2 * DEPTH) ** 0.25
LN_EPS = 1e-5
RMS_EPS = 1e-6

V7X_VMEM_BYTES = 64 * 1024 * 1024
LANES = 128


def _cparams(n_axes, vmem_mib):
    return pltpu.CompilerParams(
        dimension_semantics=("arbitrary",) * n_axes,
        vmem_limit_bytes=vmem_mib * 1024 * 1024,
    )


def _sigmoid(x):
    return 1.0 / (1.0 + jnp.exp(-x))


def _layer_norm_rows(y, g, b):
    mu = jnp.mean(y, axis=-1, keepdims=True)
    yc = y - mu
    var = jnp.mean(yc * yc, axis=-1, keepdims=True)
    return yc * lax.rsqrt(var + LN_EPS) * g + b


def _dense_kernel(x_ref, w_ref, b_ref, o_ref, wbf_ref):
    @pl.when(pl.program_id(1) == 0)
    def _():
        wbf_ref[...] = w_ref[...].astype(BF16)

    acc = jnp.dot(x_ref[...].astype(BF16), wbf_ref[...], preferred_element_type=F32)
    o_ref[...] = (acc + b_ref[...]).astype(o_ref.dtype)


def _dense(x, w, b, out_dtype, tm, tn):
    n, k = x.shape
    m = w.shape[1]
    assert n % tm == 0 and m % tn == 0
    return pl.pallas_call(
        _dense_kernel,
        out_shape=jax.ShapeDtypeStruct((n, m), out_dtype),
        grid=(m // tn, n // tm),
        in_specs=[
            pl.BlockSpec((tm, k), lambda j, i: (i, 0)),
            pl.BlockSpec((k, tn), lambda j, i: (0, j)),
            pl.BlockSpec((1, tn), lambda j, i: (0, j)),
        ],
        out_specs=pl.BlockSpec((tm, tn), lambda j, i: (i, j)),
        scratch_shapes=[pltpu.VMEM((k, tn), BF16)],
        compiler_params=_cparams(2, 48),
        name="dense",
    )(x, w, b)


def _gates_kernel(x_ref, w_ref, alog_ref, dtb_ref, o_ref):
    tm = x_ref.shape[0]
    raw = jnp.dot(x_ref[...].astype(BF16), w_ref[...].astype(BF16), preferred_element_type=F32)
    lane = lax.broadcasted_iota(I32, raw.shape, 1)
    row = lax.broadcasted_iota(I32, raw.shape, 0)
    pos = row % CHUNK
    is_beta = (lane % 32) < GDN_HV
    beta = _sigmoid(raw)
    z = raw + dtb_ref[...]
    softplus = jnp.maximum(z, 0.0) + jnp.log(1.0 + jnp.exp(-jnp.abs(z)))
    g = -jnp.exp(alog_ref[...]) * softplus
    pre = g
    suf = g
    s = 1
    while s < CHUNK:
        down = pltpu.roll(pre, s, 0)
        pre = pre + jnp.where(pos >= s, down, 0.0)
        up = pltpu.roll(suf, tm - s, 0)
        suf = suf + jnp.where(pos < CHUNK - s, up, 0.0)
        s *= 2
    o_ref[...] = jnp.where(is_beta, beta, jnp.where(lane < 32, pre, suf))


def _gates(xb, w_g, alog_row, dtb_row, tm):
    n, k = xb.shape
    return pl.pallas_call(
        _gates_kernel,
        out_shape=jax.ShapeDtypeStruct((n, 64), F32),
        grid=(n // tm,),
        in_specs=[
            pl.BlockSpec((tm, k), lambda i: (i, 0)),
            pl.BlockSpec((k, 64), lambda i: (0, 0)),
            pl.BlockSpec((1, 64), lambda i: (0, 0)),
            pl.BlockSpec((1, 64), lambda i: (0, 0)),
        ],
        out_specs=pl.BlockSpec((tm, 64), lambda i: (i, 0)),
        compiler_params=_cparams(1, 32),
        name="gdn_gates",
    )(xb, w_g, alog_row, dtb_row)


CONV_COLS = 512
CONV_ROWS = 256


def _conv_kernel(h_ref, cw_ref, o_ref, pad_ref):
    s = h_ref.shape[1]
    j = pl.program_id(1)
    zeros8 = jnp.zeros((8, CONV_COLS), F32)
    pad_ref[0:8, :] = zeros8
    pad_ref[s + 8:s + 16, :] = zeros8
    pad_ref[8:s + 8, :] = h_ref[0].astype(F32)
    w = cw_ref[...]
    n_q_blocks = QK_DIM // CONV_COLS
    for r in range(s // CONV_ROWS):
        r0 = r * CONV_ROWS
        acc = jnp.zeros((CONV_ROWS, CONV_COLS), F32)
        for d in range(CONV_W):
            acc = acc + pad_ref[r0 + 6 + d:r0 + 6 + d + CONV_ROWS, :] * w[d:d + 1, :]
        y = acc * _sigmoid(acc)

        @pl.when(j >= 2 * n_q_blocks)
        def _():
            o_ref[0, r0:r0 + CONV_ROWS, :] = y.astype(o_ref.dtype)

        @pl.when(j < 2 * n_q_blocks)
        def _():
            scale = jnp.where(j < n_q_blocks, GDN_DK ** -0.5, 1.0).astype(F32)
            for hh in range(CONV_COLS // GDN_DK):
                yh = y[:, hh * GDN_DK:(hh + 1) * GDN_DK]
                inv = lax.rsqrt(jnp.sum(yh * yh, axis=-1, keepdims=True) + RMS_EPS) * scale
                o_ref[0, r0:r0 + CONV_ROWS, hh * GDN_DK:(hh + 1) * GDN_DK] = (yh * inv).astype(o_ref.dtype)


def _conv(hq, conv_w):
    bsz, s, _ = hq.shape
    return pl.pallas_call(
        _conv_kernel,
        out_shape=jax.ShapeDtypeStruct((bsz, s, CONV_CH), BF16),
        grid=(bsz, CONV_CH // CONV_COLS),
        in_specs=[
            pl.BlockSpec((1, s, CONV_COLS), lambda b, j: (b, 0, j)),
            pl.BlockSpec((CONV_W, CONV_COLS), lambda b, j: (0, j)),
        ],
        out_specs=pl.BlockSpec((1, s, CONV_COLS), lambda b, j: (b, 0, j)),
        scratch_shapes=[pltpu.VMEM((s + 16, CONV_COLS), F32)],
        compiler_params=_cparams(2, 40),
        name="gdn_conv",
    )(hq, conv_w)


N_COMBO = 4


def _scan_kernel(q_ref, k_ref, v_ref, g_ref, o_ref, bc_ref, wm_ref, n_ref, e_ref, gl_ref, s_ref):
    s_len = q_ref.shape[1]
    nc = s_len // CHUNK
    hk = pl.program_id(1)
    c = CHUNK

    rows_pb = 256

    def bc_body(rb, carry):
        r0 = pl.multiple_of(rb * rows_pb, rows_pb)
        gt = g_ref[0, pl.ds(r0, rows_pb), :]
        lane = lax.broadcasted_iota(I32, gt.shape, 1)
        for m in range(N_COMBO):
            d, vh = divmod(m, 2)
            for kind in range(2):
                idx = d * 32 + kind * GDN_HV + 2 * hk + vh
                col = jnp.sum(jnp.where(lane == idx, gt, 0.0), axis=-1, keepdims=True)
                bc_ref[2 * m + kind, pl.ds(r0, rows_pb), :] = jnp.broadcast_to(col, (rows_pb, LANES))
        return carry

    lax.fori_loop(0, s_len // rows_pb, bc_body, 0)

    row = lax.broadcasted_iota(I32, (c, 4 * c), 0)
    lane = lax.broadcasted_iota(I32, (c, 4 * c), 1)
    col = lane % c
    fwd = lane < 2 * c
    ident = row == col
    tri_sign = jnp.where(fwd, row - col, col - row)
    incl = tri_sign >= 0
    strict = tri_sign > 0
    ident_f = ident.astype(F32)
    r4 = lax.broadcasted_iota(I32, (4 * c, 4 * c), 0)
    l4 = lax.broadcasted_iota(I32, (4 * c, 4 * c), 1)
    bdmask = (r4 // c) == (l4 // c)
    half = lax.broadcasted_iota(I32, (c, LANES), 1) < c

    def blockdiag(x):
        x4 = jnp.concatenate([x, x, x, x], axis=0)
        return jnp.where(bdmask, x4, 0.0).astype(BF16)

    stack_masked = blockdiag

    def mm(a, b):
        return jnp.dot(a.astype(BF16), b.astype(BF16), preferred_element_type=F32)

    def p1_body(ci, carry):
        r0 = pl.multiple_of(ci * c, c)
        kc = k_ref[0, pl.ds(r0, c), :]
        qc = q_ref[0, pl.ds(r0, c), :]
        vc = v_ref[0, pl.ds(r0, c), :].astype(F32)
        kf = kc.astype(F32)
        qf = qc.astype(F32)
        k4 = jnp.concatenate([kc, kc, kc, kc], axis=0)
        nt = (((1,), (1,)), ((), ()))
        gp = lax.dot_general(kc, k4, nt, preferred_element_type=F32)
        qp = lax.dot_general(qc, k4, nt, preferred_element_type=F32)
        beta = [bc_ref[2 * m, pl.ds(r0, c), :] for m in range(N_COMBO)]
        gcum = [bc_ref[2 * m + 1, pl.ds(r0, c), :] for m in range(N_COMBO)]
        beta_p = jnp.concatenate([jnp.where(half, beta[0], beta[1]), jnp.where(half, beta[2], beta[3])], axis=1)
        gc_p = jnp.concatenate([jnp.where(half, gcum[0], gcum[1]), jnp.where(half, gcum[2], gcum[3])], axis=1)
        gc_row = jnp.sum(jnp.where(ident, gc_p, 0.0), axis=0, keepdims=True)
        dec = jnp.exp(jnp.where(incl, gc_p - gc_row, -jnp.inf))
        a = jnp.where(strict, beta_p * gp * dec, 0.0)
        qd = qp * dec
        p = ident_f - a
        x = mm(a, blockdiag(a))
        for it in range(5):
            p = p + mm(p, blockdiag(x))
            if it < 4:
                x = mm(x, blockdiag(x))
        rhs = []
        eg = []
        for m in range(N_COMBO):
            vh = m % 2
            e_m = jnp.exp(gcum[m])
            eg.append(e_m)
            vb = vc[:, vh * GDN_DV:(vh + 1) * GDN_DV] * beta[m]
            kb = kf * (beta[m] * e_m)
            rhs.append(jnp.concatenate([vb, kb], axis=1))
        rhs = jnp.concatenate(rhs, axis=0).astype(BF16)
        sol = jnp.dot(stack_masked(p), rhs, preferred_element_type=F32)
        solb = sol.astype(BF16)
        ef = jnp.dot(stack_masked(qd), solb, preferred_element_type=F32)
        kd = []
        for m in range(N_COMBO):
            gl_row = gcum[m][c - 1:c, :] if m < 2 else gcum[m][0:1, :]
            kd.append((kf * jnp.exp(gl_row - gcum[m])).astype(BF16))
            gl_ref[ci, m] = jnp.broadcast_to(jnp.exp(gl_row), (8, LANES))
            e_m = qf * eg[m] - ef[m * c:(m + 1) * c, GDN_DV:]
            e_ref[ci, m] = e_m.astype(BF16)
        zk = jnp.zeros((c, GDN_DK), BF16)
        tn = (((0,), (0,)), ((), ()))
        for pr in range(2):
            kd_bd = jnp.concatenate(
                [jnp.concatenate([kd[2 * pr], zk], axis=1), jnp.concatenate([zk, kd[2 * pr + 1]], axis=1)], axis=0)
            mn = lax.dot_general(kd_bd, solb[pr * 2 * c:(pr + 1) * 2 * c, :], tn, preferred_element_type=F32)
            for q in range(2):
                m = 2 * pr + q
                blk = mn[q * GDN_DK:(q + 1) * GDN_DK, :]
                n_ref[ci, m] = blk[:, :GDN_DV]
                wm_ref[ci, m] = (-blk[:, GDN_DV:]).astype(BF16)
        for vh in range(2):
            o_ref[0, pl.ds(r0, c), vh * GDN_DV:(vh + 1) * GDN_DV] = ef[vh * c:(vh + 1) * c, :GDN_DV] + ef[(2 + vh) * c:(3 + vh) * c, :GDN_DV]
        return carry

    lax.fori_loop(0, nc, p1_body, 0)

    s_ref[...] = jnp.zeros_like(s_ref)

    def p2_body(i, carry):
        for m in range(N_COMBO):
            vh = m % 2
            ci = i if m < 2 else nc - 1 - i
            r0 = pl.multiple_of(ci * c, c)
            st = s_ref[m]
            lhs = jnp.concatenate([wm_ref[ci, m], e_ref[ci, m]], axis=0)
            res = jnp.dot(lhs, st.astype(BF16), preferred_element_type=F32)
            s_ref[m] = st * gl_ref[ci, m][0:1, :] + res[:GDN_DK] + n_ref[ci, m]
            o_ref[0, pl.ds(r0, c), vh * GDN_DV:(vh + 1) * GDN_DV] += res[GDN_DK:]
        return carry

    lax.fori_loop(0, nc, p2_body, 0)


def _scan(qkv, gates):
    bsz, s, _ = qkv.shape
    nc = s // CHUNK
    return pl.pallas_call(
        _scan_kernel,
        out_shape=jax.ShapeDtypeStruct((bsz, s, V_DIM), F32),
        grid=(bsz, GDN_HK),
        in_specs=[
            pl.BlockSpec((1, s, GDN_DK), lambda b, h: (b, 0, h)),
            pl.BlockSpec((1, s, GDN_DK), lambda b, h: (b, 0, GDN_HK + h)),
            pl.BlockSpec((1, s, 2 * GDN_DV), lambda b, h: (b, 0, GDN_HK + h)),
            pl.BlockSpec((1, s, 64), lambda b, h: (b, 0, 0)),
        ],
        out_specs=pl.BlockSpec((1, s, 2 * GDN_DV), lambda b, h: (b, 0, h)),
        scratch_shapes=[
            pltpu.VMEM((2 * N_COMBO, s, LANES), F32),
            pltpu.VMEM((nc, N_COMBO, GDN_DK, GDN_DV), BF16),
            pltpu.VMEM((nc, N_COMBO, GDN_DK, GDN_DV), F32),
            pltpu.VMEM((nc, N_COMBO, CHUNK, GDN_DK), BF16),
            pltpu.VMEM((nc, N_COMBO, 8, LANES), F32),
            pltpu.VMEM((N_COMBO, GDN_DK, GDN_DV), F32),
        ],
        compiler_params=_cparams(2, 56),
        name="gdn_scan",
    )(qkv, qkv, qkv, gates)


def _mix_out_kernel(*refs, gated):
    if gated:
        o_ref, z_ref, nw_ref, x_ref, w_ref, b_ref, g_ref, beta_ref, xo_ref, xb_ref, wbf_ref = refs
    else:
        o_ref, x_ref, w_ref, b_ref, g_ref, beta_ref, xo_ref, xb_ref, wbf_ref = refs

    @pl.when(pl.program_id(0) == 0)
    def _():
        wbf_ref[...] = w_ref[...].astype(BF16)

    if gated:
        parts = []
        for h in range(GDN_HV):
            sl = slice(h * GDN_DV, (h + 1) * GDN_DV)
            oh = o_ref[:, sl]
            zh = z_ref[:, sl].astype(F32)
            oh = oh * lax.rsqrt(jnp.mean(oh * oh, axis=-1, keepdims=True) + RMS_EPS) * nw_ref[...]
            parts.append((oh * (zh * _sigmoid(zh))).astype(BF16))
        act = jnp.concatenate(parts, axis=1)
    else:
        act = o_ref[...].astype(BF16)
    h = jnp.dot(act, wbf_ref[...], preferred_element_type=F32) + b_ref[...]
    y = _layer_norm_rows(ALPHA_DN * x_ref[...] + h, g_ref[...], beta_ref[...])
    xo_ref[...] = y
    xb_ref[...] = y.astype(BF16)


def _mix_out(o, x, w, b, ln_g, ln_b, tm, z_src=None, z_col_block=None, norm_w=None):
    n, kdim = o.shape
    gated = z_src is not None
    row = lambda i: (i, 0)
    const = lambda i: (0, 0)
    in_specs = [pl.BlockSpec((tm, kdim), row)]
    args = [o]
    if gated:
        in_specs += [pl.BlockSpec((tm, kdim), lambda i: (i, z_col_block)), pl.BlockSpec((1, GDN_DV), const)]
        args += [z_src, norm_w]
    in_specs += [
        pl.BlockSpec((tm, D_MODEL), row),
        pl.BlockSpec((kdim, D_MODEL), const),
        pl.BlockSpec((1, D_MODEL), const),
        pl.BlockSpec((1, D_MODEL), const),
        pl.BlockSpec((1, D_MODEL), const),
    ]
    args += [x, w, b, ln_g, ln_b]
    return pl.pallas_call(
        functools.partial(_mix_out_kernel, gated=gated),
        out_shape=(jax.ShapeDtypeStruct((n, D_MODEL), F32), jax.ShapeDtypeStruct((n, D_MODEL), BF16)),
        grid=(n // tm,),
        in_specs=in_specs,
        out_specs=(pl.BlockSpec((tm, D_MODEL), row), pl.BlockSpec((tm, D_MODEL), row)),
        scratch_shapes=[pltpu.VMEM((kdim, D_MODEL), BF16)],
        compiler_params=_cparams(1, 56),
        name="mix_out_gated" if gated else "mix_out",
    )(*args)


def _attn_kernel(sink_ref, q_ref, kp_ref, kc_ref, kn_ref, vp_ref, vc_ref, vn_ref, o_ref):
    j = pl.program_id(1)
    nb = pl.num_programs(1)
    qi = lax.broadcasted_iota(I32, (WB, 3 * WB), 0)
    sj = lax.broadcasted_iota(I32, (WB, 3 * WB), 1)
    dist = jnp.abs(qi + WB - sj)
    key_blk = j - 1 + sj // WB
    valid = (dist <= WINDOW) & (key_blk >= 0) & (key_blk < nb)
    dist_f = dist.astype(F32)
    lane_lo = lax.broadcasted_iota(I32, (1, LANES), 1) < ATT_DH
    scale = ATT_DH ** -0.5
    nt = (((1,), (1,)), ((), ()))

    kf = jnp.concatenate([kp_ref[0], kc_ref[0], kn_ref[0]], axis=0).astype(F32)
    vf = jnp.concatenate([vp_ref[0], vc_ref[0], vn_ref[0]], axis=0).astype(F32)
    for pair in range(ATT_HKV // 2):
        kpair = kf[:, pair * LANES:(pair + 1) * LANES]
        vpair = vf[:, pair * LANES:(pair + 1) * LANES]
        kroll = pltpu.roll(kpair, ATT_DH, 1)
        vroll = pltpu.roll(vpair, ATT_DH, 1)
        for hsub in range(2):
            hkv = 2 * pair + hsub
            for qt in range(2):
                tile = 2 * hkv + qt
                qtile = q_ref[0, :, tile * LANES:(tile + 1) * LANES].astype(F32)
                acc = jnp.zeros((WB, LANES), F32)
                for qh in range(2):
                    head = 2 * tile + qh
                    in_lo = qh == 0
                    qmask = lane_lo if in_lo else jnp.logical_not(lane_lo)
                    qm = jnp.where(qmask, qtile, 0.0).astype(BF16)
                    aligned = hsub == qh
                    kk = (kpair if aligned else kroll).astype(BF16)
                    vv = jnp.where(qmask, vpair if aligned else vroll, 0.0).astype(BF16)
                    slope = 2.0 ** (-8.0 * (head + 1) / ATT_HQ)
                    sc = lax.dot_general(qm, kk, nt, preferred_element_type=F32) * scale - slope * dist_f
                    sc = jnp.where(valid, sc, -jnp.inf)
                    sink = sink_ref[head]
                    mx = jnp.maximum(jnp.max(sc, axis=-1, keepdims=True), sink)
                    p = jnp.exp(sc - mx)
                    denom = jnp.sum(p, axis=-1, keepdims=True) + jnp.exp(sink - mx)
                    acc = acc + jnp.dot(p.astype(BF16), vv, preferred_element_type=F32) / denom
                o_ref[0, :, tile * LANES:(tile + 1) * LANES] = acc.astype(o_ref.dtype)


def _attn(hb, sinks):
    bsz, s, _ = hb.shape
    nb = s // WB
    kcol = Q_DIM // KV_DIM
    vcol = kcol + 1
    prev = lambda b, j: jnp.maximum(j - 1, 0)
    nxt = lambda b, j: jnp.minimum(j + 1, nb - 1)
    return pl.pallas_call(
        _attn_kernel,
        out_shape=jax.ShapeDtypeStruct((bsz, s, Q_DIM), BF16),
        grid=(bsz, nb),
        in_specs=[
            pl.BlockSpec(memory_space=pltpu.SMEM),
            pl.BlockSpec((1, WB, Q_DIM), lambda b, j: (b, j, 0)),
            pl.BlockSpec((1, WB, KV_DIM), lambda b, j: (b, prev(b, j), kcol)),
            pl.BlockSpec((1, WB, KV_DIM), lambda b, j: (b, j, kcol)),
            pl.BlockSpec((1, WB, KV_DIM), lambda b, j: (b, nxt(b, j), kcol)),
            pl.BlockSpec((1, WB, KV_DIM), lambda b, j: (b, prev(b, j), vcol)),
            pl.BlockSpec((1, WB, KV_DIM), lambda b, j: (b, j, vcol)),
            pl.BlockSpec((1, WB, KV_DIM), lambda b, j: (b, nxt(b, j), vcol)),
        ],
        out_specs=pl.BlockSpec((1, WB, Q_DIM), lambda b, j: (b, j, 0)),
        compiler_params=_cparams(2, 32),
        name="win_attn",
    )(sinks, hb, hb, hb, hb, hb, hb, hb)


def _split_bf16(a):
    hi = a.astype(BF16)
    lo = (a - hi.astype(F32)).astype(BF16)
    return hi, lo


def _router_kernel(x_ref, w_ref, b_ref, e_ref, r_ref, g_ref, cnt_ref, carry_ref):
    tm = x_ref.shape[0]

    @pl.when(pl.program_id(0) == 0)
    def _():
        carry_ref[...] = jnp.zeros_like(carry_ref)

    xh, xl = _split_bf16(x_ref[...])
    wh, wl = _split_bf16(w_ref[...])
    logits = (jnp.dot(xh, wh, preferred_element_type=F32) + jnp.dot(xh, wl, preferred_element_type=F32)
              + jnp.dot(xl, wh, preferred_element_type=F32)) + b_ref[...]
    lane = lax.broadcasted_iota(I32, (tm, N_EXPERTS), 1)
    out_lane = lax.broadcasted_iota(I32, (tm, LANES), 1)
    work = logits
    sel = jnp.zeros((tm, N_EXPERTS), F32)
    tops, onehots = [], []
    e_out = jnp.zeros((tm, LANES), I32)
    for k in range(TOP_K):
        mx = jnp.max(work, axis=-1, keepdims=True)
        idx = jnp.min(jnp.where(work == mx, lane, N_EXPERTS), axis=-1, keepdims=True)
        oh = lane == idx
        work = jnp.where(oh, -jnp.inf, work)
        sel = sel + oh.astype(F32)
        tops.append(mx)
        onehots.append(oh)
        e_out = jnp.where(out_lane == k, idx, e_out)
    exps = [jnp.exp(t - tops[0]) for t in tops]
    denom = exps[0] + exps[1] + exps[2] + exps[3]
    ri = lax.broadcasted_iota(I32, (tm, tm), 0)
    ci = lax.broadcasted_iota(I32, (tm, tm), 1)
    tri = (ci <= ri).astype(BF16)
    prefix = jnp.dot(tri, sel.astype(BF16), preferred_element_type=F32)
    rank_all = carry_ref[0:1, 0:N_EXPERTS] + prefix - sel
    g_out = jnp.zeros((tm, LANES), F32)
    r_out = jnp.zeros((tm, LANES), I32)
    for k in range(TOP_K):
        rk = jnp.sum(jnp.where(onehots[k], rank_all, 0.0), axis=-1, keepdims=True)
        r_out = jnp.where(out_lane == k, rk.astype(I32), r_out)
        g_out = jnp.where(out_lane == k, exps[k] / denom, g_out)
    e_ref[...] = e_out
    r_ref[...] = r_out
    g_ref[...] = g_out
    new_cnt = carry_ref[0:1, 0:N_EXPERTS] + prefix[tm - 1:tm, :]
    carry_ref[0:1, 0:N_EXPERTS] = new_cnt
    cnt_ref[...] = jnp.broadcast_to(new_cnt, (8, N_EXPERTS)).astype(I32)


def _router(x, rw, rb, tm):
    n = x.shape[0]
    row = lambda i: (i, 0)
    const = lambda i: (0, 0)
    return pl.pallas_call(
        _router_kernel,
        out_shape=(
            jax.ShapeDtypeStruct((n, LANES), I32),
            jax.ShapeDtypeStruct((n, LANES), I32),
            jax.ShapeDtypeStruct((n, LANES), F32),
            jax.ShapeDtypeStruct((8, N_EXPERTS), I32),
        ),
        grid=(n // tm,),
        in_specs=[
            pl.BlockSpec((tm, D_MODEL), row),
            pl.BlockSpec((D_MODEL, N_EXPERTS), const),
            pl.BlockSpec((1, N_EXPERTS), const),
        ],
        out_specs=(
            pl.BlockSpec((tm, LANES), row),
            pl.BlockSpec((tm, LANES), row),
            pl.BlockSpec((tm, LANES), row),
            pl.BlockSpec((8, N_EXPERTS), const),
        ),
        scratch_shapes=[pltpu.VMEM((8, LANES), F32)],
        compiler_params=_cparams(1, 32),
        name="moe_router",
    )(x, rw, rb)


DISPATCH_TM = 256


def _dispatch_kernel(pend_ref, dest_ref, x_ref, xs_ref, zero_ref, zsem, rsem):
    tm = x_ref.shape[0]
    n_pad = xs_ref.shape[0]

    def zero_copy(e):
        start = pl.multiple_of(jnp.clip(pend_ref[e] - MOE_BLOCK, 0, n_pad - MOE_BLOCK), MOE_BLOCK)
        return pltpu.make_async_copy(zero_ref, xs_ref.at[pl.ds(start, MOE_BLOCK)], zsem)

    n_blocks = n_pad // MOE_BLOCK
    n_used = pend_ref[N_EXPERTS - 1] // MOE_BLOCK

    def tail_copy(q):
        blk = jnp.minimum(n_used + q, n_blocks - 1)
        return pltpu.make_async_copy(zero_ref, xs_ref.at[pl.ds(pl.multiple_of(blk * MOE_BLOCK, MOE_BLOCK), MOE_BLOCK)], zsem)

    @pl.when(pl.program_id(0) == 0)
    def _():
        zero_ref[...] = jnp.zeros_like(zero_ref)
        def nonempty(e):
            return pend_ref[e] > (pend_ref[e - 1] if e > 0 else 0)

        for e in range(N_EXPERTS):
            @pl.when(nonempty(e))
            def _():
                zero_copy(e).start()
        for e in range(N_EXPERTS):
            @pl.when(nonempty(e))
            def _():
                zero_copy(e).wait()
        for q in range(N_EXPERTS):
            @pl.when(n_used + q < n_blocks)
            def _():
                tail_copy(q).start()
        for q in range(N_EXPERTS):
            @pl.when(n_used + q < n_blocks)
            def _():
                tail_copy(q).wait()

    def row_copy(t, k):
        d = dest_ref[t * TOP_K + k]
        return pltpu.make_async_copy(x_ref.at[pl.ds(t, 1)], xs_ref.at[pl.ds(d, 1)], rsem)

    def issue(t, carry):
        for k in range(TOP_K):
            row_copy(t, k).start()
        return carry

    lax.fori_loop(0, tm, issue, 0, unroll=8)

    def drain(t, carry):
        for k in range(TOP_K):
            row_copy(t, k).wait()
        return carry

    lax.fori_loop(0, tm, drain, 0, unroll=8)


def _dispatch(x, dest_flat, pends, n_pad):
    n = x.shape[0]
    tm = DISPATCH_TM
    return pl.pallas_call(
        _dispatch_kernel,
        out_shape=jax.ShapeDtypeStruct((n_pad, D_MODEL), F32),
        grid_spec=pltpu.PrefetchScalarGridSpec(
            num_scalar_prefetch=1,
            grid=(n // tm,),
            in_specs=[
                pl.BlockSpec((tm * TOP_K,), lambda i, pend: (i,), memory_space=pltpu.SMEM),
                pl.BlockSpec((tm, D_MODEL), lambda i, pend: (i, 0)),
            ],
            out_specs=pl.BlockSpec(memory_space=pl.ANY),
            scratch_shapes=[
                pltpu.VMEM((MOE_BLOCK, D_MODEL), F32),
                pltpu.SemaphoreType.DMA(()),
                pltpu.SemaphoreType.DMA(()),
            ],
        ),
        compiler_params=_cparams(1, 32),
        name="moe_dispatch",
    )(pends, dest_flat, x)


def _expert_kernel(be_ref, nused_ref, xs_ref, wu_ref, bu_ref, wd_ref, bd_ref, y_ref, wub_ref, wdb_ref):
    i = pl.program_id(0)
    prev_e = be_ref[jnp.maximum(i - 1, 0)]
    changed = (i == 0) | (be_ref[i] != prev_e)

    @pl.when(changed)
    def _():
        wub_ref[...] = wu_ref[0].astype(BF16)
        wdb_ref[...] = wd_ref[0].astype(BF16)

    @pl.when(i < nused_ref[0])
    def _():
        xb = xs_ref[...].astype(BF16)
        hb = jnp.dot(xb, wub_ref[...], preferred_element_type=F32) + bu_ref[0]
        gate = jnp.minimum(hb[:, :D_FF], SWIGLU_LIMIT)
        up = jnp.clip(hb[:, D_FF:], -SWIGLU_LIMIT, SWIGLU_LIMIT)
        act = gate * _sigmoid(SWIGLU_ALPHA * gate) * (up + 1.0)
        y = jnp.dot(act.astype(BF16), wdb_ref[...], preferred_element_type=F32) + bd_ref[0]
        y_ref[...] = y

    @pl.when(i >= nused_ref[0])
    def _():
        y_ref[...] = jnp.zeros_like(y_ref)


def _experts(xs, blk_e, n_used, w_up, b_up, w_down, b_down):
    n_pad = xs.shape[0]
    n_blocks = n_pad // MOE_BLOCK
    blk = lambda i, be, nu: (i, 0)
    wsel = lambda i, be, nu: (be[i], 0, 0)
    return pl.pallas_call(
        _expert_kernel,
        out_shape=jax.ShapeDtypeStruct((n_pad, D_MODEL), F32),
        grid_spec=pltpu.PrefetchScalarGridSpec(
            num_scalar_prefetch=2,
            grid=(n_blocks,),
            in_specs=[
                pl.BlockSpec((MOE_BLOCK, D_MODEL), blk),
                pl.BlockSpec((1, D_MODEL, 2 * D_FF), wsel),
                pl.BlockSpec((1, 1, 2 * D_FF), wsel),
                pl.BlockSpec((1, D_FF, D_MODEL), wsel),
                pl.BlockSpec((1, 1, D_MODEL), wsel),
            ],
            out_specs=pl.BlockSpec((MOE_BLOCK, D_MODEL), blk),
            scratch_shapes=[pltpu.VMEM((D_MODEL, 2 * D_FF), BF16), pltpu.VMEM((D_FF, D_MODEL), BF16)],
        ),
        compiler_params=_cparams(1, 56),
        name="moe_experts",
    )(blk_e, n_used, xs, w_up, b_up, w_down, b_down)


COMBINE_TM = 256


def _combine_kernel(dest_ref, g_ref, x_ref, ys_ref, lg_ref, lb_ref, xo_ref, xb_ref, buf_ref, sem):
    tm = x_ref.shape[0]

    def row_copy(t, k):
        d = dest_ref[t * TOP_K + k]
        return pltpu.make_async_copy(ys_ref.at[pl.ds(d, 1)], buf_ref.at[k, pl.ds(t, 1)], sem)

    def issue(t, carry):
        for k in range(TOP_K):
            row_copy(t, k).start()
        return carry

    lax.fori_loop(0, tm, issue, 0, unroll=8)

    def drain(t, carry):
        for k in range(TOP_K):
            row_copy(t, k).wait()
        return carry

    lax.fori_loop(0, tm, drain, 0, unroll=8)

    gates = g_ref[...]
    h = jnp.zeros((tm, D_MODEL), F32)
    for k in range(TOP_K):
        h = h + buf_ref[k] * gates[:, k:k + 1]
    y = _layer_norm_rows(ALPHA_DN * x_ref[...] + h, lg_ref[...], lb_ref[...])
    xo_ref[...] = y
    xb_ref[...] = y.astype(BF16)


def _combine(dest_flat, gates, x, ys, ln_g, ln_b):
    n = x.shape[0]
    tm = COMBINE_TM
    row = lambda i: (i, 0)
    const = lambda i: (0, 0)
    return pl.pallas_call(
        _combine_kernel,
        out_shape=(jax.ShapeDtypeStruct((n, D_MODEL), F32), jax.ShapeDtypeStruct((n, D_MODEL), BF16)),
        grid=(n // tm,),
        in_specs=[
            pl.BlockSpec((tm * TOP_K,), lambda i: (i,), memory_space=pltpu.SMEM),
            pl.BlockSpec((tm, LANES), row),
            pl.BlockSpec((tm, D_MODEL), row),
            pl.BlockSpec(memory_space=pl.ANY),
            pl.BlockSpec((1, D_MODEL), const),
            pl.BlockSpec((1, D_MODEL), const),
        ],
        out_specs=(pl.BlockSpec((tm, D_MODEL), row), pl.BlockSpec((tm, D_MODEL), row)),
        scratch_shapes=[pltpu.VMEM((TOP_K, tm, D_MODEL), F32), pltpu.SemaphoreType.DMA(())],
        compiler_params=_cparams(1, 32),
        name="moe_combine",
    )(dest_flat, gates, x, ys, ln_g, ln_b)


def _row(v):
    return v.reshape(1, -1).astype(F32)


def _deltanet_layer(x, xb, w_in, conv_w, a_log, dt_bias, norm_w, w_out, ln_g, ln_b, bsz, s):
    n = bsz * s
    hq = _dense(xb, w_in[:, :CONV_CH + V_DIM], jnp.zeros((1, CONV_CH + V_DIM), F32), BF16, 512, 2048)
    pad16 = jnp.zeros((2, GDN_HV), F32)
    alog_row = jnp.concatenate([pad16[0], a_log[0], pad16[0], a_log[1]]).reshape(1, 64)
    dtb_row = jnp.concatenate([pad16[0], dt_bias[0], pad16[0], dt_bias[1]]).reshape(1, 64)
    gates = _gates(xb, w_in[:, CONV_CH + V_DIM:], alog_row, dtb_row, 512)
    hq3 = hq.reshape(bsz, s, CONV_CH + V_DIM)
    qkv = _conv(hq3, conv_w)
    o = _scan(qkv, gates.reshape(bsz, s, 64))
    return _mix_out(o.reshape(n, V_DIM), x, w_out, jnp.zeros((1, D_MODEL), F32), _row(ln_g), _row(ln_b), 256,
                    z_src=hq, z_col_block=CONV_CH // V_DIM, norm_w=_row(norm_w))


def _attention_layer(x, xb, w_in, b_in, sinks, w_out, b_out, ln_g, ln_b, bsz, s):
    n = bsz * s
    hb = _dense(xb, w_in, _row(b_in), BF16, 512, Q_DIM + 2 * KV_DIM)
    o = _attn(hb.reshape(bsz, s, Q_DIM + 2 * KV_DIM), sinks.astype(F32))
    return _mix_out(o.reshape(n, Q_DIM), x, w_out, _row(b_out), _row(ln_g), _row(ln_b), 256)


def _moe_layer(x, router_w, router_b, w_up, b_up, w_down, b_down, ln_g, ln_b):
    n = x.shape[0]
    n_rows = n * TOP_K
    n_blocks = (n_rows + MOE_BLOCK - 1) // MOE_BLOCK + N_EXPERTS
    n_pad = n_blocks * MOE_BLOCK
    top_e, rank, gates, cnt = _router(x, router_w, _row(router_b), 512)
    counts = cnt[0]
    padded = (counts + MOE_BLOCK - 1) // MOE_BLOCK * MOE_BLOCK
    pends = jnp.cumsum(padded).astype(I32)
    pstarts = pends - padded
    e4 = top_e[:, :TOP_K]
    onehot = e4[:, :, None] == jnp.arange(N_EXPERTS, dtype=I32)[None, None, :]
    dest = jnp.sum(jnp.where(onehot, pstarts[None, None, :], 0), axis=-1) + rank[:, :TOP_K]
    dest_flat = dest.reshape(n_rows).astype(I32)
    blk_start = jnp.arange(n_blocks, dtype=I32) * MOE_BLOCK
    blk_e = jnp.minimum(jnp.sum((blk_start[:, None] >= pends[None, :]).astype(I32), axis=1), N_EXPERTS - 1).astype(I32)
    n_used = (pends[N_EXPERTS - 1:] // MOE_BLOCK).astype(I32)
    xs = _dispatch(x, dest_flat, pends, n_pad)
    ys = _experts(xs, blk_e, n_used, w_up, b_up.reshape(N_EXPERTS, 1, 2 * D_FF), w_down,
                  b_down.reshape(N_EXPERTS, 1, D_MODEL))
    return _combine(dest_flat, gates, x, ys, _row(ln_g), _row(ln_b))


def kernel(x, a_w_in, a_conv_w, a_A_log, a_dt_bias, a_norm_w, a_w_out, b_w_in, b_b_in, b_sinks, b_w_out, b_b_out,
           router_w, router_b, exp_w_up, exp_b_up, exp_w_down, exp_b_down, ln_g, ln_b):
    bsz, s, d = x.shape
    n = bsz * s
    xf = x.reshape(n, d)
    xb = xf.astype(BF16)
    for i in range(DEPTH):
        j = i // 2
        if i % 2 == 0:
            xf, xb = _deltanet_layer(xf, xb, a_w_in[j], a_conv_w[j], a_A_log[j], a_dt_bias[j], a_norm_w[j],
                                     a_w_out[j], ln_g[i, 0], ln_b[i, 0], bsz, s)
        else:
            xf, xb = _attention_layer(xf, xb, b_w_in[j], b_b_in[j], b_sinks[j], b_w_out[j], b_b_out[j],
                                      ln_g[i, 0], ln_b[i, 0], bsz, s)
        xf, xb = _moe_layer(xf, router_w[i], router_b[i], exp_w_up[i], exp_b_up[i], exp_w_down[i], exp_b_down[i],
                            ln_g[i, 1], ln_b[i, 1])
    return xf.reshape(bsz, s, d)
```

```python
import functools

import jax
import jax.numpy as jnp
from jax import lax
from jax.experimental import pallas as pl
from jax.experimental.pallas import tpu as pltpu

F32 = jnp.float32
BF16 = jnp.bfloat16
I32 = jnp.int32

D_MODEL = 1024
DEPTH = 4
GDN_DK = 128
GDN_DV = 128
GDN_HK = 8
GDN_HV = 16
QK_DIM = GDN_HK * GDN_DK
V_DIM = GDN_HV * GDN_DV
CONV_W = 5
CONV_CH = 2 * QK_DIM + V_DIM
CHUNK = 64
ATT_DH = 64
ATT_HQ = 16
ATT_HKV = 4
ATT_G = ATT_HQ // ATT_HKV
Q_DIM = ATT_HQ * ATT_DH
KV_DIM = ATT_HKV * ATT_DH
WB = 128
WINDOW = 128
N_EXPERTS = 32
TOP_K = 4
D_FF = D_MODEL
SWIGLU_LIMIT = 7.0
SWIGLU_ALPHA = 1.702
MOE_BLOCK = 256
ALPHA_DN = (2 * DEPTH) ** 0.25
LN_EPS = 1e-5
RMS_EPS = 1e-6

V7X_VMEM_BYTES = 64 * 1024 * 1024
LANES = 128


def _cparams(n_axes, vmem_mib):
    return pltpu.CompilerParams(
        dimension_semantics=("arbitrary",) * n_axes,
        vmem_limit_bytes=vmem_mib * 1024 * 1024,
    )


def _sigmoid(x):
    return 1.0 / (1.0 + jnp.exp(-x))


def _layer_norm_rows(y, g, b):
    mu = jnp.mean(y, axis=-1, keepdims=True)
    yc = y - mu
    var = jnp.mean(yc * yc, axis=-1, keepdims=True)
    return yc * lax.rsqrt(var + LN_EPS) * g + b


def _dense_kernel(x_ref, w_ref, b_ref, o_ref, wbf_ref):
    @pl.when(pl.program_id(1) == 0)
    def _():
        wbf_ref[...] = w_ref[...].astype(BF16)

    acc = jnp.dot(x_ref[...].astype(BF16), wbf_ref[...], preferred_element_type=F32)
    o_ref[...] = (acc + b_ref[...]).astype(o_ref.dtype)


def _dense(x, w, b, out_dtype, tm, tn):
    n, k = x.shape
    m = w.shape[1]
    assert n % tm == 0 and m % tn == 0
    return pl.pallas_call(
        _dense_kernel,
        out_shape=jax.ShapeDtypeStruct((n, m), out_dtype),
        grid=(m // tn, n // tm),
        in_specs=[
            pl.BlockSpec((tm, k), lambda j, i: (i, 0)),
            pl.BlockSpec((k, tn), lambda j, i: (0, j)),
            pl.BlockSpec((1, tn), lambda j, i: (0, j)),
        ],
        out_specs=pl.BlockSpec((tm, tn), lambda j, i: (i, j)),
        scratch_shapes=[pltpu.VMEM((k, tn), BF16)],
        compiler_params=_cparams(2, 48),
        name="dense",
    )(x, w, b)


def _gates_kernel(x_ref, w_ref, alog_ref, dtb_ref, o_ref):
    tm = x_ref.shape[0]
    raw = jnp.dot(x_ref[...].astype(BF16), w_ref[...].astype(BF16), preferred_element_type=F32)
    lane = lax.broadcasted_iota(I32, raw.shape, 1)
    row = lax.broadcasted_iota(I32, raw.shape, 0)
    pos = row % CHUNK
    is_beta = (lane % 32) < GDN_HV
    beta = _sigmoid(raw)
    z = raw + dtb_ref[...]
    softplus = jnp.maximum(z, 0.0) + jnp.log(1.0 + jnp.exp(-jnp.abs(z)))
    g = -jnp.exp(alog_ref[...]) * softplus
    pre = g
    suf = g
    s = 1
    while s < CHUNK:
        down = pltpu.roll(pre, s, 0)
        pre = pre + jnp.where(pos >= s, down, 0.0)
        up = pltpu.roll(suf, tm - s, 0)
        suf = suf + jnp.where(pos < CHUNK - s, up, 0.0)
        s *= 2
    o_ref[...] = jnp.where(is_beta, beta, jnp.where(lane < 32, pre, suf))


def _gates(xb, w_g, alog_row, dtb_row, tm):
    n, k = xb.shape
    return pl.pallas_call(
        _gates_kernel,
        out_shape=jax.ShapeDtypeStruct((n, 64), F32),
        grid=(n // tm,),
        in_specs=[
            pl.BlockSpec((tm, k), lambda i: (i, 0)),
            pl.BlockSpec((k, 64), lambda i: (0, 0)),
            pl.BlockSpec((1, 64), lambda i: (0, 0)),
            pl.BlockSpec((1, 64), lambda i: (0, 0)),
        ],
        out_specs=pl.BlockSpec((tm, 64), lambda i: (i, 0)),
        compiler_params=_cparams(1, 32),
        name="gdn_gates",
    )(xb, w_g, alog_row, dtb_row)


CONV_COLS = 512
CONV_ROWS = 256


def _conv_kernel(h_ref, cw_ref, o_ref, pad_ref):
    s = h_ref.shape[1]
    j = pl.program_id(1)
    zeros8 = jnp.zeros((8, CONV_COLS), F32)
    pad_ref[0:8, :] = zeros8
    pad_ref[s + 8:s + 16, :] = zeros8
    pad_ref[8:s + 8, :] = h_ref[0].astype(F32)
    w = cw_ref[...]
    n_q_blocks = QK_DIM // CONV_COLS
    for r in range(s // CONV_ROWS):
        r0 = r * CONV_ROWS
        acc = jnp.zeros((CONV_ROWS, CONV_COLS), F32)
        for d in range(CONV_W):
            acc = acc + pad_ref[r0 + 6 + d:r0 + 6 + d + CONV_ROWS, :] * w[d:d + 1, :]
        y = acc * _sigmoid(acc)

        @pl.when(j >= 2 * n_q_blocks)
        def _():
            o_ref[0, r0:r0 + CONV_ROWS, :] = y.astype(o_ref.dtype)

        @pl.when(j < 2 * n_q_blocks)
        def _():
            scale = jnp.where(j < n_q_blocks, GDN_DK ** -0.5, 1.0).astype(F32)
            for hh in range(CONV_COLS // GDN_DK):
                yh = y[:, hh * GDN_DK:(hh + 1) * GDN_DK]
                inv = lax.rsqrt(jnp.sum(yh * yh, axis=-1, keepdims=True) + RMS_EPS) * scale
                o_ref[0, r0:r0 + CONV_ROWS, hh * GDN_DK:(hh + 1) * GDN_DK] = (yh * inv).astype(o_ref.dtype)


def _conv(hq, conv_w):
    bsz, s, _ = hq.shape
    return pl.pallas_call(
        _conv_kernel,
        out_shape=jax.ShapeDtypeStruct((bsz, s, CONV_CH), BF16),
        grid=(bsz, CONV_CH // CONV_COLS),
        in_specs=[
            pl.BlockSpec((1, s, CONV_COLS), lambda b, j: (b, 0, j)),
            pl.BlockSpec((CONV_W, CONV_COLS), lambda b, j: (0, j)),
        ],
        out_specs=pl.BlockSpec((1, s, CONV_COLS), lambda b, j: (b, 0, j)),
        scratch_shapes=[pltpu.VMEM((s + 16, CONV_COLS), F32)],
        compiler_params=_cparams(2, 40),
        name="gdn_conv",
    )(hq, conv_w)


N_COMBO = 4
P1_UNROLL = 8


def _scan_kernel(q_ref, k_ref, v_ref, g_ref, o_ref, bc_ref, wm_ref, n_ref, e_ref, gl_ref, s_ref):
    s_len = q_ref.shape[1]
    nc = s_len // CHUNK
    hk = pl.program_id(1)
    c = CHUNK

    rows_pb = 256

    def bc_body(rb, carry):
        r0 = pl.multiple_of(rb * rows_pb, rows_pb)
        gt = g_ref[0, pl.ds(r0, rows_pb), :]
        lane = lax.broadcasted_iota(I32, gt.shape, 1)
        for m in range(N_COMBO):
            d, vh = divmod(m, 2)
            for kind in range(2):
                idx = d * 32 + kind * GDN_HV + 2 * hk + vh
                col = jnp.sum(jnp.where(lane == idx, gt, 0.0), axis=-1, keepdims=True)
                bc_ref[2 * m + kind, pl.ds(r0, rows_pb), :] = jnp.broadcast_to(col, (rows_pb, LANES))
        return carry

    lax.fori_loop(0, s_len // rows_pb, bc_body, 0)

    row = lax.broadcasted_iota(I32, (c, 4 * c), 0)
    lane = lax.broadcasted_iota(I32, (c, 4 * c), 1)
    col = lane % c
    fwd = lane < 2 * c
    ident = row == col
    tri_sign = jnp.where(fwd, row - col, col - row)
    incl = tri_sign >= 0
    strict = tri_sign > 0
    ident_f = ident.astype(F32)
    r4 = lax.broadcasted_iota(I32, (4 * c, 4 * c), 0)
    l4 = lax.broadcasted_iota(I32, (4 * c, 4 * c), 1)
    bdmask = (r4 // c) == (l4 // c)
    half = lax.broadcasted_iota(I32, (c, LANES), 1) < c

    def blockdiag(x):
        x4 = jnp.concatenate([x, x, x, x], axis=0)
        return jnp.where(bdmask, x4, 0.0).astype(BF16)

    stack_masked = blockdiag

    def mm(a, b):
        return jnp.dot(a.astype(BF16), b.astype(BF16), preferred_element_type=F32)

    def p1_body(ci):
        r0 = pl.multiple_of(ci * c, c)
        kc = k_ref[0, pl.ds(r0, c), :]
        qc = q_ref[0, pl.ds(r0, c), :]
        vc = v_ref[0, pl.ds(r0, c), :].astype(F32)
        kf = kc.astype(F32)
        qf = qc.astype(F32)
        k4 = jnp.concatenate([kc, kc, kc, kc], axis=0)
        nt = (((1,), (1,)), ((), ()))
        gp = lax.dot_general(kc, k4, nt, preferred_element_type=F32)
        qp = lax.dot_general(qc, k4, nt, preferred_element_type=F32)
        yield
        beta = [bc_ref[2 * m, pl.ds(r0, c), :] for m in range(N_COMBO)]
        gcum = [bc_ref[2 * m + 1, pl.ds(r0, c), :] for m in range(N_COMBO)]
        beta_p = jnp.concatenate([jnp.where(half, beta[0], beta[1]), jnp.where(half, beta[2], beta[3])], axis=1)
        gc_p = jnp.concatenate([jnp.where(half, gcum[0], gcum[1]), jnp.where(half, gcum[2], gcum[3])], axis=1)
        gc_row = jnp.sum(jnp.where(ident, gc_p, 0.0), axis=0, keepdims=True)
        dec = jnp.exp(jnp.where(incl, gc_p - gc_row, -jnp.inf))
        a = jnp.where(strict, beta_p * gp * dec, 0.0)
        qd = qp * dec
        p = ident_f - a
        x = mm(a, blockdiag(a))
        yield
        for it in range(5):
            p = p + mm(p, blockdiag(x))
            if it < 4:
                x = mm(x, blockdiag(x))
            yield
        rhs = []
        eg = []
        for m in range(N_COMBO):
            vh = m % 2
            e_m = jnp.exp(gcum[m])
            eg.append(e_m)
            vb = vc[:, vh * GDN_DV:(vh + 1) * GDN_DV] * beta[m]
            kb = kf * (beta[m] * e_m)
            rhs.append(jnp.concatenate([vb, kb], axis=1))
        rhs = jnp.concatenate(rhs, axis=0).astype(BF16)
        sol = jnp.dot(stack_masked(p), rhs, preferred_element_type=F32)
        yield
        solb = sol.astype(BF16)
        ef = jnp.dot(stack_masked(qd), solb, preferred_element_type=F32)
        yield
        kd = []
        stores = []
        for m in range(N_COMBO):
            gl_row = gcum[m][c - 1:c, :] if m < 2 else gcum[m][0:1, :]
            kd.append((kf * jnp.exp(gl_row - gcum[m])).astype(BF16))
            stores.append((gl_ref, m, jnp.broadcast_to(jnp.exp(gl_row), (8, LANES))))
            e_m = qf * eg[m] - ef[m * c:(m + 1) * c, GDN_DV:]
            stores.append((e_ref, m, e_m.astype(BF16)))
        zk = jnp.zeros((c, GDN_DK), BF16)
        tn = (((0,), (0,)), ((), ()))
        for pr in range(2):
            kd_bd = jnp.concatenate(
                [jnp.concatenate([kd[2 * pr], zk], axis=1), jnp.concatenate([zk, kd[2 * pr + 1]], axis=1)], axis=0)
            mn = lax.dot_general(kd_bd, solb[pr * 2 * c:(pr + 1) * 2 * c, :], tn, preferred_element_type=F32)
            for q in range(2):
                m = 2 * pr + q
                blk = mn[q * GDN_DK:(q + 1) * GDN_DK, :]
                stores.append((n_ref, m, blk[:, :GDN_DV]))
                stores.append((wm_ref, m, (-blk[:, GDN_DV:]).astype(BF16)))
        f_sum = [ef[vh * c:(vh + 1) * c, :GDN_DV] + ef[(2 + vh) * c:(3 + vh) * c, :GDN_DV] for vh in range(2)]
        return stores, f_sum

    def run_interleaved(gens):
        results = [None] * len(gens)
        live = list(range(len(gens)))
        while live:
            for u in list(live):
                try:
                    next(gens[u])
                except StopIteration as stop:
                    results[u] = stop.value
                    live.remove(u)
        return results

    def p1_group(it, carry):
        done = run_interleaved([p1_body(it * P1_UNROLL + u) for u in range(P1_UNROLL)])
        for u, (stores, f_sum) in enumerate(done):
            ci = it * P1_UNROLL + u
            r0 = pl.multiple_of(ci * c, c)
            for ref, m, val in stores:
                ref[ci, m] = val
            for vh in range(2):
                o_ref[0, pl.ds(r0, c), vh * GDN_DV:(vh + 1) * GDN_DV] = f_sum[vh]
        return carry

    lax.fori_loop(0, nc // P1_UNROLL, p1_group, 0)

    s_ref[...] = jnp.zeros_like(s_ref)

    def p2_body(i, carry):
        for m in range(N_COMBO):
            vh = m % 2
            ci = i if m < 2 else nc - 1 - i
            r0 = pl.multiple_of(ci * c, c)
            st = s_ref[m]
            lhs = jnp.concatenate([wm_ref[ci, m], e_ref[ci, m]], axis=0)
            res = jnp.dot(lhs, st.astype(BF16), preferred_element_type=F32)
            s_ref[m] = st * gl_ref[ci, m][0:1, :] + res[:GDN_DK] + n_ref[ci, m]
            o_ref[0, pl.ds(r0, c), vh * GDN_DV:(vh + 1) * GDN_DV] += res[GDN_DK:]
        return carry

    lax.fori_loop(0, nc, p2_body, 0)


def _scan(qkv, gates):
    bsz, s, _ = qkv.shape
    nc = s // CHUNK
    return pl.pallas_call(
        _scan_kernel,
        out_shape=jax.ShapeDtypeStruct((bsz, s, V_DIM), F32),
        grid=(bsz, GDN_HK),
        in_specs=[
            pl.BlockSpec((1, s, GDN_DK), lambda b, h: (b, 0, h)),
            pl.BlockSpec((1, s, GDN_DK), lambda b, h: (b, 0, GDN_HK + h)),
            pl.BlockSpec((1, s, 2 * GDN_DV), lambda b, h: (b, 0, GDN_HK + h)),
            pl.BlockSpec((1, s, 64), lambda b, h: (b, 0, 0)),
        ],
        out_specs=pl.BlockSpec((1, s, 2 * GDN_DV), lambda b, h: (b, 0, h)),
        scratch_shapes=[
            pltpu.VMEM((2 * N_COMBO, s, LANES), F32),
            pltpu.VMEM((nc, N_COMBO, GDN_DK, GDN_DV), BF16),
            pltpu.VMEM((nc, N_COMBO, GDN_DK, GDN_DV), F32),
            pltpu.VMEM((nc, N_COMBO, CHUNK, GDN_DK), BF16),
            pltpu.VMEM((nc, N_COMBO, 8, LANES), F32),
            pltpu.VMEM((N_COMBO, GDN_DK, GDN_DV), F32),
        ],
        compiler_params=_cparams(2, 56),
        name="gdn_scan",
    )(qkv, qkv, qkv, gates)


def _mix_out_kernel(*refs, gated):
    if gated:
        o_ref, z_ref, nw_ref, x_ref, w_ref, b_ref, g_ref, beta_ref, xo_ref, xb_ref, wbf_ref = refs
    else:
        o_ref, x_ref, w_ref, b_ref, g_ref, beta_ref, xo_ref, xb_ref, wbf_ref = refs

    @pl.when(pl.program_id(0) == 0)
    def _():
        wbf_ref[...] = w_ref[...].astype(BF16)

    if gated:
        parts = []
        for h in range(GDN_HV):
            sl = slice(h * GDN_DV, (h + 1) * GDN_DV)
            oh = o_ref[:, sl]
            zh = z_ref[:, sl].astype(F32)
            oh = oh * lax.rsqrt(jnp.mean(oh * oh, axis=-1, keepdims=True) + RMS_EPS) * nw_ref[...]
            parts.append((oh * (zh * _sigmoid(zh))).astype(BF16))
        act = jnp.concatenate(parts, axis=1)
    else:
        act = o_ref[...].astype(BF16)
    h = jnp.dot(act, wbf_ref[...], preferred_element_type=F32) + b_ref[...]
    y = _layer_norm_rows(ALPHA_DN * x_ref[...] + h, g_ref[...], beta_ref[...])
    xo_ref[...] = y
    xb_ref[...] = y.astype(BF16)


def _mix_out(o, x, w, b, ln_g, ln_b, tm, z_src=None, z_col_block=None, norm_w=None):
    n, kdim = o.shape
    gated = z_src is not None
    row = lambda i: (i, 0)
    const = lambda i: (0, 0)
    in_specs = [pl.BlockSpec((tm, kdim), row)]
    args = [o]
    if gated:
        in_specs += [pl.BlockSpec((tm, kdim), lambda i: (i, z_col_block)), pl.BlockSpec((1, GDN_DV), const)]
        args += [z_src, norm_w]
    in_specs += [
        pl.BlockSpec((tm, D_MODEL), row),
        pl.BlockSpec((kdim, D_MODEL), const),
        pl.BlockSpec((1, D_MODEL), const),
        pl.BlockSpec((1, D_MODEL), const),
        pl.BlockSpec((1, D_MODEL), const),
    ]
    args += [x, w, b, ln_g, ln_b]
    return pl.pallas_call(
        functools.partial(_mix_out_kernel, gated=gated),
        out_shape=(jax.ShapeDtypeStruct((n, D_MODEL), F32), jax.ShapeDtypeStruct((n, D_MODEL), BF16)),
        grid=(n // tm,),
        in_specs=in_specs,
        out_specs=(pl.BlockSpec((tm, D_MODEL), row), pl.BlockSpec((tm, D_MODEL), row)),
        scratch_shapes=[pltpu.VMEM((kdim, D_MODEL), BF16)],
        compiler_params=_cparams(1, 56),
        name="mix_out_gated" if gated else "mix_out",
    )(*args)


def _attn_kernel(sink_ref, q_ref, kp_ref, kc_ref, kn_ref, vp_ref, vc_ref, vn_ref, o_ref):
    j = pl.program_id(1)
    nb = pl.num_programs(1)
    qi = lax.broadcasted_iota(I32, (WB, 3 * WB), 0)
    sj = lax.broadcasted_iota(I32, (WB, 3 * WB), 1)
    dist = jnp.abs(qi + WB - sj)
    key_blk = j - 1 + sj // WB
    valid = (dist <= WINDOW) & (key_blk >= 0) & (key_blk < nb)
    dist_f = dist.astype(F32)
    lane_lo = lax.broadcasted_iota(I32, (1, LANES), 1) < ATT_DH
    scale = ATT_DH ** -0.5
    nt = (((1,), (1,)), ((), ()))

    kf = jnp.concatenate([kp_ref[0], kc_ref[0], kn_ref[0]], axis=0).astype(F32)
    vf = jnp.concatenate([vp_ref[0], vc_ref[0], vn_ref[0]], axis=0).astype(F32)
    for pair in range(ATT_HKV // 2):
        kpair = kf[:, pair * LANES:(pair + 1) * LANES]
        vpair = vf[:, pair * LANES:(pair + 1) * LANES]
        kroll = pltpu.roll(kpair, ATT_DH, 1)
        vroll = pltpu.roll(vpair, ATT_DH, 1)
        for hsub in range(2):
            hkv = 2 * pair + hsub
            for qt in range(2):
                tile = 2 * hkv + qt
                qtile = q_ref[0, :, tile * LANES:(tile + 1) * LANES].astype(F32)
                acc = jnp.zeros((WB, LANES), F32)
                for qh in range(2):
                    head = 2 * tile + qh
                    in_lo = qh == 0
                    qmask = lane_lo if in_lo else jnp.logical_not(lane_lo)
                    qm = jnp.where(qmask, qtile, 0.0).astype(BF16)
                    aligned = hsub == qh
                    kk = (kpair if aligned else kroll).astype(BF16)
                    vv = jnp.where(qmask, vpair if aligned else vroll, 0.0).astype(BF16)
                    slope = 2.0 ** (-8.0 * (head + 1) / ATT_HQ)
                    sc = lax.dot_general(qm, kk, nt, preferred_element_type=F32) * scale - slope * dist_f
                    sc = jnp.where(valid, sc, -jnp.inf)
                    sink = sink_ref[head]
                    mx = jnp.maximum(jnp.max(sc, axis=-1, keepdims=True), sink)
                    p = jnp.exp(sc - mx)
                    denom = jnp.sum(p, axis=-1, keepdims=True) + jnp.exp(sink - mx)
                    acc = acc + jnp.dot(p.astype(BF16), vv, preferred_element_type=F32) / denom
                o_ref[0, :, tile * LANES:(tile + 1) * LANES] = acc.astype(o_ref.dtype)


def _attn(hb, sinks):
    bsz, s, _ = hb.shape
    nb = s // WB
    kcol = Q_DIM // KV_DIM
    vcol = kcol + 1
    prev = lambda b, j: jnp.maximum(j - 1, 0)
    nxt = lambda b, j: jnp.minimum(j + 1, nb - 1)
    return pl.pallas_call(
        _attn_kernel,
        out_shape=jax.ShapeDtypeStruct((bsz, s, Q_DIM), BF16),
        grid=(bsz, nb),
        in_specs=[
            pl.BlockSpec(memory_space=pltpu.SMEM),
            pl.BlockSpec((1, WB, Q_DIM), lambda b, j: (b, j, 0)),
            pl.BlockSpec((1, WB, KV_DIM), lambda b, j: (b, prev(b, j), kcol)),
            pl.BlockSpec((1, WB, KV_DIM), lambda b, j: (b, j, kcol)),
            pl.BlockSpec((1, WB, KV_DIM), lambda b, j: (b, nxt(b, j), kcol)),
            pl.BlockSpec((1, WB, KV_DIM), lambda b, j: (b, prev(b, j), vcol)),
            pl.BlockSpec((1, WB, KV_DIM), lambda b, j: (b, j, vcol)),
            pl.BlockSpec((1, WB, KV_DIM), lambda b, j: (b, nxt(b, j), vcol)),
        ],
        out_specs=pl.BlockSpec((1, WB, Q_DIM), lambda b, j: (b, j, 0)),
        compiler_params=_cparams(2, 32),
        name="win_attn",
    )(sinks, hb, hb, hb, hb, hb, hb, hb)


def _split_bf16(a):
    hi = a.astype(BF16)
    lo = (a - hi.astype(F32)).astype(BF16)
    return hi, lo


def _router_kernel(x_ref, w_ref, b_ref, e_ref, r_ref, g_ref, cnt_ref, carry_ref):
    tm = x_ref.shape[0]

    @pl.when(pl.program_id(0) == 0)
    def _():
        carry_ref[...] = jnp.zeros_like(carry_ref)

    xh, xl = _split_bf16(x_ref[...])
    wh, wl = _split_bf16(w_ref[...])
    logits = (jnp.dot(xh, wh, preferred_element_type=F32) + jnp.dot(xh, wl, preferred_element_type=F32)
              + jnp.dot(xl, wh, preferred_element_type=F32)) + b_ref[...]
    lane = lax.broadcasted_iota(I32, (tm, N_EXPERTS), 1)
    out_lane = lax.broadcasted_iota(I32, (tm, LANES), 1)
    work = logits
    sel = jnp.zeros((tm, N_EXPERTS), F32)
    tops, onehots = [], []
    e_out = jnp.zeros((tm, LANES), I32)
    for k in range(TOP_K):
        mx = jnp.max(work, axis=-1, keepdims=True)
        idx = jnp.min(jnp.where(work == mx, lane, N_EXPERTS), axis=-1, keepdims=True)
        oh = lane == idx
        work = jnp.where(oh, -jnp.inf, work)
        sel = sel + oh.astype(F32)
        tops.append(mx)
        onehots.append(oh)
        e_out = jnp.where(out_lane == k, idx, e_out)
    exps = [jnp.exp(t - tops[0]) for t in tops]
    denom = exps[0] + exps[1] + exps[2] + exps[3]
    ri = lax.broadcasted_iota(I32, (tm, tm), 0)
    ci = lax.broadcasted_iota(I32, (tm, tm), 1)
    tri = (ci <= ri).astype(BF16)
    prefix = jnp.dot(tri, sel.astype(BF16), preferred_element_type=F32)
    rank_all = carry_ref[0:1, 0:N_EXPERTS] + prefix - sel
    g_out = jnp.zeros((tm, LANES), F32)
    r_out = jnp.zeros((tm, LANES), I32)
    for k in range(TOP_K):
        rk = jnp.sum(jnp.where(onehots[k], rank_all, 0.0), axis=-1, keepdims=True)
        r_out = jnp.where(out_lane == k, rk.astype(I32), r_out)
        g_out = jnp.where(out_lane == k, exps[k] / denom, g_out)
    e_ref[...] = e_out
    r_ref[...] = r_out
    g_ref[...] = g_out
    new_cnt = carry_ref[0:1, 0:N_EXPERTS] + prefix[tm - 1:tm, :]
    carry_ref[0:1, 0:N_EXPERTS] = new_cnt
    cnt_ref[...] = jnp.broadcast_to(new_cnt, (8, N_EXPERTS)).astype(I32)


def _router(x, rw, rb, tm):
    n = x.shape[0]
    row = lambda i: (i, 0)
    const = lambda i: (0, 0)
    return pl.pallas_call(
        _router_kernel,
        out_shape=(
            jax.ShapeDtypeStruct((n, LANES), I32),
            jax.ShapeDtypeStruct((n, LANES), I32),
            jax.ShapeDtypeStruct((n, LANES), F32),
            jax.ShapeDtypeStruct((8, N_EXPERTS), I32),
        ),
        grid=(n // tm,),
        in_specs=[
            pl.BlockSpec((tm, D_MODEL), row),
            pl.BlockSpec((D_MODEL, N_EXPERTS), const),
            pl.BlockSpec((1, N_EXPERTS), const),
        ],
        out_specs=(
            pl.BlockSpec((tm, LANES), row),
            pl.BlockSpec((tm, LANES), row),
            pl.BlockSpec((tm, LANES), row),
            pl.BlockSpec((8, N_EXPERTS), const),
        ),
        scratch_shapes=[pltpu.VMEM((8, LANES), F32)],
        compiler_params=_cparams(1, 32),
        name="moe_router",
    )(x, rw, rb)


DISPATCH_TM = 256


def _dispatch_kernel(pend_ref, dest_ref, x_ref, xs_ref, zero_ref, zsem, rsem):
    tm = x_ref.shape[0]
    n_pad = xs_ref.shape[0]

    def zero_copy(e):
        start = pl.multiple_of(jnp.clip(pend_ref[e] - MOE_BLOCK, 0, n_pad - MOE_BLOCK), MOE_BLOCK)
        return pltpu.make_async_copy(zero_ref, xs_ref.at[pl.ds(start, MOE_BLOCK)], zsem)

    n_blocks = n_pad // MOE_BLOCK
    n_used = pend_ref[N_EXPERTS - 1] // MOE_BLOCK

    def tail_copy(q):
        blk = jnp.minimum(n_used + q, n_blocks - 1)
        return pltpu.make_async_copy(zero_ref, xs_ref.at[pl.ds(pl.multiple_of(blk * MOE_BLOCK, MOE_BLOCK), MOE_BLOCK)], zsem)

    @pl.when(pl.program_id(0) == 0)
    def _():
        zero_ref[...] = jnp.zeros_like(zero_ref)
        def nonempty(e):
            return pend_ref[e] > (pend_ref[e - 1] if e > 0 else 0)

        for e in range(N_EXPERTS):
            @pl.when(nonempty(e))
            def _():
                zero_copy(e).start()
        for e in range(N_EXPERTS):
            @pl.when(nonempty(e))
            def _():
                zero_copy(e).wait()
        for q in range(N_EXPERTS):
            @pl.when(n_used + q < n_blocks)
            def _():
                tail_copy(q).start()
        for q in range(N_EXPERTS):
            @pl.when(n_used + q < n_blocks)
            def _():
                tail_copy(q).wait()

    def row_copy(t, k):
        d = dest_ref[t * TOP_K + k]
        return pltpu.make_async_copy(x_ref.at[pl.ds(t, 1)], xs_ref.at[pl.ds(d, 1)], rsem)

    def issue(t, carry):
        for k in range(TOP_K):
            row_copy(t, k).start(priority=k % 2)
        return carry

    lax.fori_loop(0, tm, issue, 0, unroll=8)

    def drain(t, carry):
        for k in range(TOP_K):
            row_copy(t, k).wait()
        return carry

    lax.fori_loop(0, tm, drain, 0, unroll=8)


def _dispatch(x, dest_flat, pends, n_pad):
    n = x.shape[0]
    tm = DISPATCH_TM
    return pl.pallas_call(
        _dispatch_kernel,
        out_shape=jax.ShapeDtypeStruct((n_pad, D_MODEL), F32),
        grid_spec=pltpu.PrefetchScalarGridSpec(
            num_scalar_prefetch=1,
            grid=(n // tm,),
            in_specs=[
                pl.BlockSpec((tm * TOP_K,), lambda i, pend: (i,), memory_space=pltpu.SMEM),
                pl.BlockSpec((tm, D_MODEL), lambda i, pend: (i, 0)),
            ],
            out_specs=pl.BlockSpec(memory_space=pl.ANY),
            scratch_shapes=[
                pltpu.VMEM((MOE_BLOCK, D_MODEL), F32),
                pltpu.SemaphoreType.DMA(()),
                pltpu.SemaphoreType.DMA(()),
            ],
        ),
        compiler_params=_cparams(1, 32),
        name="moe_dispatch",
    )(pends, dest_flat, x)


def _expert_kernel(be_ref, nused_ref, xs_ref, wu_ref, bu_ref, wd_ref, bd_ref, y_ref, wub_ref, wdb_ref):
    i = pl.program_id(0)
    prev_e = be_ref[jnp.maximum(i - 1, 0)]
    changed = (i == 0) | (be_ref[i] != prev_e)

    @pl.when(changed)
    def _():
        wub_ref[...] = wu_ref[0].astype(BF16)
        wdb_ref[...] = wd_ref[0].astype(BF16)

    @pl.when(i < nused_ref[0])
    def _():
        xb = xs_ref[...].astype(BF16)
        hb = jnp.dot(xb, wub_ref[...], preferred_element_type=F32) + bu_ref[0]
        gate = jnp.minimum(hb[:, :D_FF], SWIGLU_LIMIT)
        up = jnp.clip(hb[:, D_FF:], -SWIGLU_LIMIT, SWIGLU_LIMIT)
        act = gate * _sigmoid(SWIGLU_ALPHA * gate) * (up + 1.0)
        y = jnp.dot(act.astype(BF16), wdb_ref[...], preferred_element_type=F32) + bd_ref[0]
        y_ref[...] = y

    @pl.when(i >= nused_ref[0])
    def _():
        y_ref[...] = jnp.zeros_like(y_ref)


def _experts(xs, blk_e, n_used, w_up, b_up, w_down, b_down, layer):
    n_pad = xs.shape[0]
    n_blocks = n_pad // MOE_BLOCK
    blk = lambda i, be, nu: (i, 0)
    wsel = lambda i, be, nu: (layer * N_EXPERTS + be[i], 0, 0)
    return pl.pallas_call(
        _expert_kernel,
        out_shape=jax.ShapeDtypeStruct((n_pad, D_MODEL), F32),
        grid_spec=pltpu.PrefetchScalarGridSpec(
            num_scalar_prefetch=2,
            grid=(n_blocks,),
            in_specs=[
                pl.BlockSpec((MOE_BLOCK, D_MODEL), blk),
                pl.BlockSpec((1, D_MODEL, 2 * D_FF), wsel),
                pl.BlockSpec((1, 1, 2 * D_FF), wsel),
                pl.BlockSpec((1, D_FF, D_MODEL), wsel),
                pl.BlockSpec((1, 1, D_MODEL), wsel),
            ],
            out_specs=pl.BlockSpec((MOE_BLOCK, D_MODEL), blk),
            scratch_shapes=[pltpu.VMEM((D_MODEL, 2 * D_FF), BF16), pltpu.VMEM((D_FF, D_MODEL), BF16)],
        ),
        compiler_params=_cparams(1, 56),
        name="moe_experts",
    )(blk_e, n_used, xs, w_up, b_up, w_down, b_down)


COMBINE_TM = 256


def _combine_kernel(dest_ref, g_ref, x_ref, ys_ref, lg_ref, lb_ref, xo_ref, xb_ref, buf_ref, sem):
    tm = x_ref.shape[0]

    def row_copy(t, k):
        d = dest_ref[t * TOP_K + k]
        return pltpu.make_async_copy(ys_ref.at[pl.ds(d, 1)], buf_ref.at[k, pl.ds(t, 1)], sem)

    def issue(t, carry):
        for k in range(TOP_K):
            row_copy(t, k).start(priority=k % 2)
        return carry

    lax.fori_loop(0, tm, issue, 0, unroll=8)

    def drain(t, carry):
        for k in range(TOP_K):
            row_copy(t, k).wait()
        return carry

    lax.fori_loop(0, tm, drain, 0, unroll=8)

    gates = g_ref[...]
    h = jnp.zeros((tm, D_MODEL), F32)
    for k in range(TOP_K):
        h = h + buf_ref[k] * gates[:, k:k + 1]
    y = _layer_norm_rows(ALPHA_DN * x_ref[...] + h, lg_ref[...], lb_ref[...])
    xo_ref[...] = y
    xb_ref[...] = y.astype(BF16)


def _combine(dest_flat, gates, x, ys, ln_g, ln_b):
    n = x.shape[0]
    tm = COMBINE_TM
    row = lambda i: (i, 0)
    const = lambda i: (0, 0)
    return pl.pallas_call(
        _combine_kernel,
        out_shape=(jax.ShapeDtypeStruct((n, D_MODEL), F32), jax.ShapeDtypeStruct((n, D_MODEL), BF16)),
        grid=(n // tm,),
        in_specs=[
            pl.BlockSpec((tm * TOP_K,), lambda i: (i,), memory_space=pltpu.SMEM),
            pl.BlockSpec((tm, LANES), row),
            pl.BlockSpec((tm, D_MODEL), row),
            pl.BlockSpec(memory_space=pl.ANY),
            pl.BlockSpec((1, D_MODEL), const),
            pl.BlockSpec((1, D_MODEL), const),
        ],
        out_specs=(pl.BlockSpec((tm, D_MODEL), row), pl.BlockSpec((tm, D_MODEL), row)),
        scratch_shapes=[pltpu.VMEM((TOP_K, tm, D_MODEL), F32), pltpu.SemaphoreType.DMA(())],
        compiler_params=_cparams(1, 32),
        name="moe_combine",
    )(dest_flat, gates, x, ys, ln_g, ln_b)


def _row(v):
    return v.reshape(1, -1).astype(F32)


def _deltanet_layer(x, xb, w_in, conv_w, a_log, dt_bias, norm_w, w_out, ln_g, ln_b, bsz, s):
    n = bsz * s
    hq = _dense(xb, w_in[:, :CONV_CH + V_DIM], jnp.zeros((1, CONV_CH + V_DIM), F32), BF16, 512, 2048)
    pad16 = jnp.zeros((2, GDN_HV), F32)
    alog_row = jnp.concatenate([pad16[0], a_log[0], pad16[0], a_log[1]]).reshape(1, 64)
    dtb_row = jnp.concatenate([pad16[0], dt_bias[0], pad16[0], dt_bias[1]]).reshape(1, 64)
    gates = _gates(xb, w_in[:, CONV_CH + V_DIM:], alog_row, dtb_row, 512)
    hq3 = hq.reshape(bsz, s, CONV_CH + V_DIM)
    qkv = _conv(hq3, conv_w)
    o = _scan(qkv, gates.reshape(bsz, s, 64))
    return _mix_out(o.reshape(n, V_DIM), x, w_out, jnp.zeros((1, D_MODEL), F32), _row(ln_g), _row(ln_b), 256,
                    z_src=hq, z_col_block=CONV_CH // V_DIM, norm_w=_row(norm_w))


def _attention_layer(x, xb, w_in, b_in, sinks, w_out, b_out, ln_g, ln_b, bsz, s):
    n = bsz * s
    hb = _dense(xb, w_in, _row(b_in), BF16, 512, Q_DIM + 2 * KV_DIM)
    o = _attn(hb.reshape(bsz, s, Q_DIM + 2 * KV_DIM), sinks.astype(F32))
    return _mix_out(o.reshape(n, Q_DIM), x, w_out, _row(b_out), _row(ln_g), _row(ln_b), 256)


def _moe_layer(x, router_w, router_b, w_up, b_up, w_down, b_down, ln_g, ln_b, layer):
    n = x.shape[0]
    n_rows = n * TOP_K
    n_blocks = (n_rows + MOE_BLOCK - 1) // MOE_BLOCK + N_EXPERTS
    n_pad = n_blocks * MOE_BLOCK
    top_e, rank, gates, cnt = _router(x, router_w, _row(router_b), 512)
    counts = cnt[0]
    padded = (counts + MOE_BLOCK - 1) // MOE_BLOCK * MOE_BLOCK
    pends = jnp.cumsum(padded).astype(I32)
    pstarts = pends - padded
    e4 = top_e[:, :TOP_K]
    onehot = e4[:, :, None] == jnp.arange(N_EXPERTS, dtype=I32)[None, None, :]
    dest = jnp.sum(jnp.where(onehot, pstarts[None, None, :], 0), axis=-1) + rank[:, :TOP_K]
    dest_flat = dest.reshape(n_rows).astype(I32)
    blk_start = jnp.arange(n_blocks, dtype=I32) * MOE_BLOCK
    blk_e = jnp.minimum(jnp.sum((blk_start[:, None] >= pends[None, :]).astype(I32), axis=1), N_EXPERTS - 1).astype(I32)
    n_used = (pends[N_EXPERTS - 1:] // MOE_BLOCK).astype(I32)
    xs = _dispatch(x, dest_flat, pends, n_pad)
    ys = _experts(xs, blk_e, n_used, w_up, b_up, w_down, b_down, layer)
    return _combine(dest_flat, gates, x, ys, _row(ln_g), _row(ln_b))


def kernel(x, a_w_in, a_conv_w, a_A_log, a_dt_bias, a_norm_w, a_w_out, b_w_in, b_b_in, b_sinks, b_w_out, b_b_out,
           router_w, router_b, exp_w_up, exp_b_up, exp_w_down, exp_b_down, ln_g, ln_b):
    bsz, s, d = x.shape
    n = bsz * s
    xf = x.reshape(n, d)
    xb = xf.astype(BF16)
    n_le = DEPTH * N_EXPERTS
    w_up_all = exp_w_up.reshape(n_le, D_MODEL, 2 * D_FF)
    b_up_all = exp_b_up.reshape(n_le, 1, 2 * D_FF)
    w_down_all = exp_w_down.reshape(n_le, D_FF, D_MODEL)
    b_down_all = exp_b_down.reshape(n_le, 1, D_MODEL)
    for i in range(DEPTH):
        j = i // 2
        if i % 2 == 0:
            xf, xb = _deltanet_layer(xf, xb, a_w_in[j], a_conv_w[j], a_A_log[j], a_dt_bias[j], a_norm_w[j],
                                     a_w_out[j], ln_g[i, 0], ln_b[i, 0], bsz, s)
        else:
            xf, xb = _attention_layer(xf, xb, b_w_in[j], b_b_in[j], b_sinks[j], b_w_out[j], b_b_out[j],
                                      ln_g[i, 0], ln_b[i, 0], bsz, s)
        xf, xb = _moe_layer(xf, router_w[i], router_b[i], w_up_all, b_up_all, w_down_all, b_down_all,
                            ln_g[i, 1], ln_b[i, 1], i)
    return xf.reshape(bsz, s, d)
```
